```python
import math
import jax, jax.numpy as jnp
from jax import lax
import numpy as np

D_MODEL = 4096
BATCH = 4
SEQ = 4096
DEPTH = 4

CHUNK = 64
N_MIXERS = 2
N_A = (DEPTH + 1) // 2
N_B = DEPTH // 2
GMLP_BLOCK = 128
GMLP_WIDTH = D_MODEL
GMLP_GROUPS = 32
GMLP_GROUP_DIM = GMLP_WIDTH // GMLP_GROUPS
SB_HEADS = 32
SB_HEAD_DIM = D_MODEL // SB_HEADS
SB_QBLOCK = 128
FFN_HIDDEN = int(math.ceil(8 * D_MODEL / 3 / 256)) * 256
RMS_EPS = 1e-6

kernel_name = "hybrid_gmlp_stickbreaking_trunk"


def rms_norm(x, g):
    xf = x.astype(jnp.float32)
    y = xf * lax.rsqrt(jnp.mean(xf * xf, axis=-1, keepdims=True) + RMS_EPS)
    return (y * g.astype(jnp.float32)).astype(x.dtype)


def gmlp_mixer(h, w_in, v_norm, w_s, b_s, w_out):
    B, S, _ = h.shape
    z = jax.nn.gelu(h @ w_in, approximate=False)
    u, v = z[..., :GMLP_WIDTH], z[..., GMLP_WIDTH:]
    v = rms_norm(v, v_norm)
    n_blk = S // GMLP_BLOCK
    vr = v.reshape(B, n_blk, GMLP_BLOCK, GMLP_GROUPS, GMLP_GROUP_DIM)
    pos = jnp.arange(GMLP_BLOCK)
    mask = (pos[None, :] // CHUNK) <= (pos[:, None] // CHUNK)
    w = jnp.where(mask[None], w_s, jnp.zeros_like(w_s))
    sv = jnp.einsum('gts,bnsgc->bntgc', w.astype(vr.dtype), vr)
    sv = sv + b_s.T.astype(sv.dtype)[None, None, :, :, None]
    gated = u * sv.reshape(B, S, GMLP_WIDTH)
    return gated @ w_out


def stick_breaking_attention(q, k, v):
    S = q.shape[2]
    scale = 1.0 / math.sqrt(SB_HEAD_DIM)
    outs = []
    for blk in range(S // SB_QBLOCK):
        q0 = blk * SB_QBLOCK
        kl = q0 + SB_QBLOCK
        qb = q[:, :, q0:kl].astype(jnp.float32)
        kb = k[:, :, :kl].astype(jnp.float32)
        vb = v[:, :, :kl]
        z = jnp.einsum('bhtd,bhsd->bhts', qb, kb) * scale
        t_idx = q0 + jnp.arange(SB_QBLOCK)[:, None]
        s_idx = jnp.arange(kl)[None, :]
        strict = s_idx < t_idx
        log_keep = jnp.where(strict, jax.nn.log_sigmoid(-z), 0.0)
        suffix = lax.cumsum(log_keep, axis=3, reverse=True) - log_keep
        log_a = jax.nn.log_sigmoid(z) + suffix
        a = jnp.where(strict, jnp.exp(log_a), 0.0)
        outs.append(jnp.einsum('bhts,bhsd->bhtd', a.astype(vb.dtype), vb))
    return jnp.concatenate(outs, axis=2)


def sb_mixer(h, w_qkv, q_norm, k_norm, w_out):
    B, S, _ = h.shape
    qkv = (h @ w_qkv).reshape(B, S, 3, SB_HEADS, SB_HEAD_DIM)
    q = rms_norm(qkv[:, :, 0], q_norm)
    k = rms_norm(qkv[:, :, 1], k_norm)
    v = qkv[:, :, 2]
    q, k, v = (t.transpose(0, 2, 1, 3) for t in (q, k, v))
    o = stick_breaking_attention(q, k, v)
    o = o.transpose(0, 2, 1, 3).reshape(B, S, SB_HEADS * SB_HEAD_DIM)
    return o @ w_out


def swiglu_ffn(h, w_gate, w_up, w_down):
    return (jax.nn.silu(h @ w_gate) * (h @ w_up)) @ w_down


def setup_inputs(seed: int = 0) -> dict:
    key = jax.random.key(seed)
    ks = jax.random.split(key, 17)

    def nrm(k, shape, scale):
        return jax.random.normal(k, shape, jnp.float32) * scale

    def gain(k, shape):
        return 1.0 + 0.02 * jax.random.normal(k, shape, jnp.float32)

    D, F = D_MODEL, FFN_HIDDEN
    return {
        "x": nrm(ks[0], (BATCH, SEQ, D), 1.0),
        "a_norm": gain(ks[1], (N_A, D)),
        "a_w_in": nrm(ks[2], (N_A, D, 2 * GMLP_WIDTH), D ** -0.5),
        "a_v_norm": gain(ks[3], (N_A, GMLP_WIDTH)),
        "a_w_s": nrm(ks[4], (N_A, GMLP_GROUPS, GMLP_BLOCK, GMLP_BLOCK), GMLP_BLOCK ** -0.5),
        "a_b_s": 1.0 + 0.02 * jax.random.normal(ks[5], (N_A, GMLP_GROUPS, GMLP_BLOCK), jnp.float32),
        "a_w_out": nrm(ks[6], (N_A, GMLP_WIDTH, D), GMLP_WIDTH ** -0.5),
        "b_norm": gain(ks[7], (N_B, D)),
        "b_w_qkv": nrm(ks[8], (N_B, D, 3 * SB_HEADS * SB_HEAD_DIM), D ** -0.5),
        "b_q_norm": gain(ks[9], (N_B, SB_HEAD_DIM)),
        "b_k_norm": gain(ks[10], (N_B, SB_HEAD_DIM)),
        "b_w_out": nrm(ks[11], (N_B, SB_HEADS * SB_HEAD_DIM, D), (SB_HEADS * SB_HEAD_DIM) ** -0.5),
        "f_norm": gain(ks[12], (DEPTH, D)),
        "f_w_gate": nrm(ks[13], (DEPTH, D, F), D ** -0.5),
        "f_w_up": nrm(ks[14], (DEPTH, D, F), D ** -0.5),
        "f_w_down": nrm(ks[15], (DEPTH, F, D), F ** -0.5),
    }


def reference(x, a_norm, a_w_in, a_v_norm, a_w_s, a_b_s, a_w_out,
              b_norm, b_w_qkv, b_q_norm, b_k_norm, b_w_out,
              f_norm, f_w_gate, f_w_up, f_w_down):
    for i in range(DEPTH):
        j = i // N_MIXERS
        if i % N_MIXERS == 0:
            h = rms_norm(x, a_norm[j])
            x = x + gmlp_mixer(h, a_w_in[j], a_v_norm[j], a_w_s[j], a_b_s[j], a_w_out[j])
        else:
            h = rms_norm(x, b_norm[j])
            x = x + sb_mixer(h, b_w_qkv[j], b_q_norm[j], b_k_norm[j], b_w_out[j])
        h = rms_norm(x, f_norm[i])
        x = x + swiglu_ffn(h, f_w_gate[i], f_w_up[i], f_w_down[i])
    return x
```

```python
import functools
import math

import jax
import jax.numpy as jnp
from jax import lax
from jax.experimental import pallas as pl
from jax.experimental.pallas import tpu as pltpu

RMS_EPS = 1e-6
CHUNK = 64
GMLP_BLOCK = 128
HEAD_DIM = 128
LANE = 128
V7X_VMEM_BYTES = 64 * 1024 * 1024
VMEM_LIMIT_BYTES = V7X_VMEM_BYTES - 8 * 1024 * 1024

_BF16 = jnp.bfloat16
_F32 = jnp.float32


def _params(*semantics):
    return pltpu.CompilerParams(dimension_semantics=semantics,
                                vmem_limit_bytes=VMEM_LIMIT_BYTES)


def _rmsnorm_kernel(x_ref, g_ref, o_ref):
    x = x_ref[...]
    ms = jnp.mean(x * x, axis=-1, keepdims=True)
    o_ref[...] = (x * lax.rsqrt(ms + RMS_EPS) * g_ref[...]).astype(o_ref.dtype)


def _rmsnorm(x, gain, *, rows=512):
    m, d = x.shape
    rows = min(rows, m)
    return pl.pallas_call(
        _rmsnorm_kernel,
        grid=(m // rows,),
        in_specs=[pl.BlockSpec((rows, d), lambda i: (i, 0)),
                  pl.BlockSpec((1, d), lambda i: (0, 0))],
        out_specs=pl.BlockSpec((rows, d), lambda i: (i, 0)),
        out_shape=jax.ShapeDtypeStruct((m, d), _BF16),
        compiler_params=_params("parallel"),
        name="rmsnorm",
    )(x, gain.reshape(1, d))


def _dot(a, b):
    return jnp.dot(a, b, preferred_element_type=_F32)


def _gelu_kernel(x_ref, w_ref, o_ref):
    acc = _dot(x_ref[...], w_ref[...])
    o_ref[...] = (0.5 * acc * (1.0 + lax.erf(acc * math.sqrt(0.5)))).astype(o_ref.dtype)


def _residual_kernel(x_ref, w_ref, r_ref, o_ref):
    o_ref[...] = r_ref[...] + _dot(x_ref[...], w_ref[...])


def _swiglu_kernel(x_ref, wg_ref, wu_ref, o_ref):
    x = x_ref[...]
    g = _dot(x, wg_ref[...])
    u = _dot(x, wu_ref[...])
    o_ref[...] = (g * jax.nn.sigmoid(g) * u).astype(o_ref.dtype)


def _qkv_kernel(x_ref, w_ref, g_ref, o_ref, *, n_norm_blocks):
    acc = _dot(x_ref[...], w_ref[...])
    j = pl.program_id(1)

    @pl.when(j < n_norm_blocks)
    def _():
        gain = g_ref[0]
        for h in range(acc.shape[1] // HEAD_DIM):
            blk = acc[:, h * HEAD_DIM:(h + 1) * HEAD_DIM]
            ms = jnp.mean(blk * blk, axis=-1, keepdims=True)
            o_ref[:, h * HEAD_DIM:(h + 1) * HEAD_DIM] = (
                blk * lax.rsqrt(ms + RMS_EPS) * gain).astype(o_ref.dtype)

    @pl.when(j >= n_norm_blocks)
    def _():
        o_ref[...] = acc.astype(o_ref.dtype)


def _proj_gelu(x, w, *, bm=1024, bn=1024):
    m, k = x.shape
    n = w.shape[1]
    bm, bn = min(bm, m), min(bn, n)
    return pl.pallas_call(
        _gelu_kernel,
        grid=(m // bm, n // bn),
        in_specs=[pl.BlockSpec((bm, k), lambda i, j: (i, 0)),
                  pl.BlockSpec((k, bn), lambda i, j: (0, j))],
        out_specs=pl.BlockSpec((bm, bn), lambda i, j: (i, j)),
        out_shape=jax.ShapeDtypeStruct((m, n), _BF16),
        compiler_params=_params("parallel", "arbitrary"),
        name="proj_gelu",
    )(x, w)


def _proj_residual(x, w, res, *, bm, bn):
    m, k = x.shape
    n = w.shape[1]
    bm, bn = min(bm, m), min(bn, n)
    return pl.pallas_call(
        _residual_kernel,
        grid=(m // bm, n // bn),
        in_specs=[pl.BlockSpec((bm, k), lambda i, j: (i, 0)),
                  pl.BlockSpec((k, bn), lambda i, j: (0, j)),
                  pl.BlockSpec((bm, bn), lambda i, j: (i, j))],
        out_specs=pl.BlockSpec((bm, bn), lambda i, j: (i, j)),
        out_shape=jax.ShapeDtypeStruct((m, n), _F32),
        compiler_params=_params("parallel", "arbitrary"),
        name="proj_residual",
    )(x, w, res)


def _proj_swiglu(x, wg, wu, *, bm=1024, bn=256):
    m, k = x.shape
    n = wg.shape[1]
    bm, bn = min(bm, m), min(bn, n)
    return pl.pallas_call(
        _swiglu_kernel,
        grid=(m // bm, n // bn),
        in_specs=[pl.BlockSpec((bm, k), lambda i, j: (i, 0)),
                  pl.BlockSpec((k, bn), lambda i, j: (0, j)),
                  pl.BlockSpec((k, bn), lambda i, j: (0, j))],
        out_specs=pl.BlockSpec((bm, bn), lambda i, j: (i, j)),
        out_shape=jax.ShapeDtypeStruct((m, n), _BF16),
        compiler_params=_params("parallel", "arbitrary"),
        name="proj_swiglu",
    )(x, wg, wu)


def _proj_qkv(x, w, q_gain, k_gain, *, bm=1024, bn=1024):
    m, k = x.shape
    n = w.shape[1]
    third = n // 3
    bm, bn = min(bm, m), min(bn, third)
    per_third = third // bn
    gains = jnp.stack([q_gain, k_gain]).reshape(2, 1, HEAD_DIM)
    return pl.pallas_call(
        functools.partial(_qkv_kernel, n_norm_blocks=2 * per_third),
        grid=(m // bm, n // bn),
        in_specs=[pl.BlockSpec((bm, k), lambda i, j: (i, 0)),
                  pl.BlockSpec((k, bn), lambda i, j: (0, j)),
                  pl.BlockSpec((1, 1, HEAD_DIM),
                               lambda i, j: (jnp.minimum(j // per_third, 1), 0, 0))],
        out_specs=pl.BlockSpec((bm, bn), lambda i, j: (i, j)),
        out_shape=jax.ShapeDtypeStruct((m, n), _BF16),
        compiler_params=_params("parallel", "arbitrary"),
        name="proj_qkv",
    )(x, w, gains)


def _spatial_kernel(u_ref, v_ref, vg_ref, w_ref, b_ref, o_ref):
    rows, width = v_ref.shape
    v = v_ref[...].astype(_F32)
    ms = jnp.mean(v * v, axis=-1, keepdims=True)
    vn = (v * lax.rsqrt(ms + RMS_EPS) * vg_ref[...]).astype(_BF16)
    t_chunk = lax.broadcasted_iota(jnp.int32, (GMLP_BLOCK, GMLP_BLOCK), 0) // CHUNK
    s_chunk = lax.broadcasted_iota(jnp.int32, (GMLP_BLOCK, GMLP_BLOCK), 1) // CHUNK
    causal = s_chunk <= t_chunk
    for g in range(width // HEAD_DIM):
        cols = slice(g * HEAD_DIM, (g + 1) * HEAD_DIM)
        w = jnp.where(causal, w_ref[g], jnp.zeros_like(w_ref[g]))
        for p in range(rows // GMLP_BLOCK):
            rws = slice(p * GMLP_BLOCK, (p + 1) * GMLP_BLOCK)
            sv = _dot(w, vn[rws, cols]) + b_ref[:, cols]
            o_ref[rws, cols] = (u_ref[rws, cols].astype(_F32) * sv).astype(o_ref.dtype)


def _spatial_gate(z, v_gain, w_s, b_s, *, rows=256):
    m, two_w = z.shape
    width = two_w // 2
    rows = min(rows, m)
    groups = width // HEAD_DIM
    b_full = jnp.repeat(b_s.T, HEAD_DIM, axis=1)
    return pl.pallas_call(
        _spatial_kernel,
        grid=(m // rows,),
        in_specs=[pl.BlockSpec((rows, width), lambda i: (i, 0)),
                  pl.BlockSpec((rows, width), lambda i: (i, 1)),
                  pl.BlockSpec((1, width), lambda i: (0, 0)),
                  pl.BlockSpec((groups, GMLP_BLOCK, GMLP_BLOCK), lambda i: (0, 0, 0)),
                  pl.BlockSpec((GMLP_BLOCK, width), lambda i: (0, 0))],
        out_specs=pl.BlockSpec((rows, width), lambda i: (i, 0)),
        out_shape=jax.ShapeDtypeStruct((m, width), _BF16),
        compiler_params=_params("parallel"),
        name="gmlp_spatial",
    )(z, z, v_gain.reshape(1, width), w_s, b_full)


def _sb_attn_kernel(q_ref, k_ref, v_ref, cs_ref, o_ref, acc_ref, carry_ref, *, tq, tk, scale):
    qi = pl.program_id(2)
    q = q_ref[...]
    cs = cs_ref[...]
    acc_ref[...] = jnp.zeros_like(acc_ref)
    carry_ref[...] = jnp.zeros_like(carry_ref)
    groups = tk // LANE

    def tile(k0, strict):
        k = k_ref[pl.ds(k0, tk), :]
        v = v_ref[pl.ds(k0, tk), :]
        z = lax.dot_general(q, k, (((1,), (1,)), ((), ())), preferred_element_type=_F32) * scale
        sp = jnp.maximum(z, 0.0) + jnp.log(1.0 + jnp.exp(-jnp.abs(z)))
        log_beta = z - sp
        if strict is not None:
            sp = jnp.where(strict, sp, 0.0)
        carry = carry_ref[...]
        a_parts = [None] * groups
        for g in reversed(range(groups)):
            cols = slice(g * LANE, (g + 1) * LANE)
            spg = sp[:, cols]
            hi = spg.astype(_BF16)
            lo = (spg - hi.astype(_F32)).astype(_BF16)
            r = _dot(jnp.concatenate([hi, lo], axis=1), cs)
            a = jnp.exp(log_beta[:, cols] - r[:, :LANE] - carry)
            if strict is not None:
                a = jnp.where(strict[:, cols], a, 0.0)
            a_parts[g] = a.astype(_BF16)
            carry = carry + r[:, LANE:]
        carry_ref[...] = carry
        acc_ref[...] += _dot(jnp.concatenate(a_parts, axis=1), v)

    t_idx = lax.broadcasted_iota(jnp.int32, (tq, tk), 0)
    s_idx = lax.broadcasted_iota(jnp.int32, (tq, tk), 1)
    n_diag = tq // tk
    for d in reversed(range(n_diag)):
        tile(pl.multiple_of(qi * tq + d * tk, tk), (s_idx + d * tk) < t_idx)

    n_full = qi * n_diag

    def body(i, c):
        tile(pl.multiple_of((n_full - 1 - i) * tk, tk), None)
        return c

    lax.fori_loop(0, n_full, body, 0)
    o_ref[...] = acc_ref[...].astype(o_ref.dtype)


def _suffix_sum_matrix():
    j = lax.broadcasted_iota(jnp.int32, (LANE, LANE), 0)
    s = lax.broadcasted_iota(jnp.int32, (LANE, LANE), 1)
    half = jnp.concatenate([(j > s).astype(_BF16), jnp.ones((LANE, LANE), _BF16)], axis=1)
    return jnp.concatenate([half, half], axis=0)


def _sb_attention(qkv, batch, seq, *, tq=256, tk=256):
    m, three_hd = qkv.shape
    heads = three_hd // (3 * HEAD_DIM)
    tq, tk = min(tq, seq), min(tk, seq)
    nq = seq // tq
    kernel = functools.partial(_sb_attn_kernel, tq=tq, tk=tk, scale=1.0 / math.sqrt(HEAD_DIM))
    return pl.pallas_call(
        kernel,
        grid=(batch, heads, nq),
        in_specs=[pl.BlockSpec((tq, HEAD_DIM), lambda b, h, i: (b * nq + i, h)),
                  pl.BlockSpec((seq, HEAD_DIM), lambda b, h, i: (b, heads + h)),
                  pl.BlockSpec((seq, HEAD_DIM), lambda b, h, i: (b, 2 * heads + h)),
                  pl.BlockSpec((2 * LANE, 2 * LANE), lambda b, h, i: (0, 0))],
        out_specs=pl.BlockSpec((tq, HEAD_DIM), lambda b, h, i: (b * nq + i, h)),
        out_shape=jax.ShapeDtypeStruct((m, heads * HEAD_DIM), _BF16),
        scratch_shapes=[pltpu.VMEM((tq, HEAD_DIM), _F32), pltpu.VMEM((tq, LANE), _F32)],
        compiler_params=_params("parallel", "parallel", "arbitrary"),
        name="sb_attention",
    )(qkv, qkv, qkv, _suffix_sum_matrix())


def kernel(x, a_norm, a_w_in, a_v_norm, a_w_s, a_b_s, a_w_out, b_norm, b_w_qkv, b_q_norm, b_k_norm, b_w_out, f_norm, f_w_gate, f_w_up, f_w_down):
    batch, seq, d = x.shape
    depth = f_norm.shape[0]
    xr = x.reshape(batch * seq, d)
    for i in range(depth):
        j = i // 2
        if i % 2 == 0:
            h = _rmsnorm(xr, a_norm[j])
            z = _proj_gelu(h, a_w_in[j].astype(_BF16))
            gated = _spatial_gate(z, a_v_norm[j], a_w_s[j].astype(_BF16), a_b_s[j])
            xr = _proj_residual(gated, a_w_out[j].astype(_BF16), xr, bm=1024, bn=1024)
        else:
            h = _rmsnorm(xr, b_norm[j])
            qkv = _proj_qkv(h, b_w_qkv[j].astype(_BF16), b_q_norm[j], b_k_norm[j])
            o = _sb_attention(qkv, batch, seq)
            xr = _proj_residual(o, b_w_out[j].astype(_BF16), xr, bm=1024, bn=1024)
        h = _rmsnorm(xr, f_norm[i])
        a = _proj_swiglu(h, f_w_gate[i].astype(_BF16), f_w_up[i].astype(_BF16))
        xr = _proj_residual(a, f_w_down[i].astype(_BF16), xr, bm=512, bn=512)
    return xr.reshape(batch, seq, d)
```

```python
import functools
import math

import jax
import jax.numpy as jnp
from jax import lax
from jax.experimental import pallas as pl
from jax.experimental.pallas import tpu as pltpu

RMS_EPS = 1e-6
CHUNK = 64
GMLP_BLOCK = 128
HEAD_DIM = 128
LANE = 128
MXU_COLS = 256
BF16_SUBLANES = 16
V7X_VMEM_BYTES = 64 * 1024 * 1024
VMEM_LIMIT_BYTES = V7X_VMEM_BYTES - 8 * 1024 * 1024
LOG2E = 1.4426950408889634
EXIT_LOG2 = 151.0

_BF16 = jnp.bfloat16
_F32 = jnp.float32


def _params(*semantics):
    return pltpu.CompilerParams(dimension_semantics=semantics,
                                vmem_limit_bytes=VMEM_LIMIT_BYTES)


def _cast_kernel(w_ref, o_ref):
    o_ref[...] = w_ref[...].astype(o_ref.dtype)


def _to_bf16(w, *, block_bytes=4 * 1024 * 1024):
    layers, k, n = w.shape
    bk = BF16_SUBLANES
    while 2 * bk * n * 4 <= block_bytes and k % (2 * bk) == 0:
        bk *= 2
    assert k % bk == 0, (k, bk)
    return pl.pallas_call(
        _cast_kernel,
        grid=(layers, k // bk),
        in_specs=[pl.BlockSpec((None, bk, n), lambda l, i: (l, i, 0))],
        out_specs=pl.BlockSpec((None, bk, n), lambda l, i: (l, i, 0)),
        out_shape=jax.ShapeDtypeStruct(w.shape, _BF16),
        compiler_params=_params("parallel", "parallel"),
        name="cast_bf16",
    )(w)


def _rmsnorm_kernel(x_ref, g_ref, o_ref):
    x = x_ref[...]
    ms = jnp.mean(x * x, axis=-1, keepdims=True)
    o_ref[...] = (x * lax.rsqrt(ms + RMS_EPS) * g_ref[...]).astype(o_ref.dtype)


def _rmsnorm(x, gain, *, rows=512):
    m, d = x.shape
    rows = min(rows, m)
    return pl.pallas_call(
        _rmsnorm_kernel,
        grid=(m // rows,),
        in_specs=[pl.BlockSpec((rows, d), lambda i: (i, 0)),
                  pl.BlockSpec((1, d), lambda i: (0, 0))],
        out_specs=pl.BlockSpec((rows, d), lambda i: (i, 0)),
        out_shape=jax.ShapeDtypeStruct((m, d), _BF16),
        compiler_params=_params("parallel"),
        name="rmsnorm",
    )(x, gain.reshape(1, d))


def _dot(a, b):
    return jnp.dot(a, b, preferred_element_type=_F32)


def _w_spec(k, bn, layer):
    return pl.BlockSpec((None, k, bn), lambda i, j: (layer, 0, j))


def _gelu_kernel(x_ref, w_ref, o_ref):
    acc = _dot(x_ref[...], w_ref[...])
    o_ref[...] = (0.5 * acc * (1.0 + lax.erf(acc * math.sqrt(0.5)))).astype(o_ref.dtype)


def _residual_kernel(x_ref, w_ref, r_ref, o_ref):
    o_ref[...] = r_ref[...] + _dot(x_ref[...], w_ref[...])


def _swiglu_kernel(x_ref, wg_ref, wu_ref, o_ref):
    x = x_ref[...]
    g = _dot(x, wg_ref[...])
    u = _dot(x, wu_ref[...])
    o_ref[...] = (g * jax.nn.sigmoid(g) * u).astype(o_ref.dtype)


def _qkv_kernel(x_ref, w_ref, g_ref, o_ref, *, n_norm_blocks):
    normed_block = pl.program_id(1) < n_norm_blocks
    gain = g_ref[0]
    x = x_ref[...]
    for c in range(o_ref.shape[1] // MXU_COLS):
        acc = _dot(x, w_ref[:, c * MXU_COLS:(c + 1) * MXU_COLS])
        for h in range(MXU_COLS // HEAD_DIM):
            blk = acc[:, h * HEAD_DIM:(h + 1) * HEAD_DIM]
            ms = jnp.mean(blk * blk, axis=-1, keepdims=True)
            normed = blk * lax.rsqrt(ms + RMS_EPS) * gain
            lo = c * MXU_COLS + h * HEAD_DIM
            o_ref[:, lo:lo + HEAD_DIM] = jnp.where(normed_block, normed, blk).astype(o_ref.dtype)


def _proj_gelu(x, w, layer, *, bm=1024, bn=1024):
    m, k = x.shape
    n = w.shape[2]
    bm, bn = min(bm, m), min(bn, n)
    return pl.pallas_call(
        _gelu_kernel,
        grid=(m // bm, n // bn),
        in_specs=[pl.BlockSpec((bm, k), lambda i, j: (i, 0)), _w_spec(k, bn, layer)],
        out_specs=pl.BlockSpec((bm, bn), lambda i, j: (i, j)),
        out_shape=jax.ShapeDtypeStruct((m, n), _BF16),
        compiler_params=_params("parallel", "arbitrary"),
        name="proj_gelu",
    )(x, w)


def _proj_residual(x, w, layer, res, *, bm, bn):
    m, k = x.shape
    n = w.shape[2]
    bm, bn = min(bm, m), min(bn, n)
    return pl.pallas_call(
        _residual_kernel,
        grid=(m // bm, n // bn),
        in_specs=[pl.BlockSpec((bm, k), lambda i, j: (i, 0)), _w_spec(k, bn, layer),
                  pl.BlockSpec((bm, bn), lambda i, j: (i, j))],
        out_specs=pl.BlockSpec((bm, bn), lambda i, j: (i, j)),
        out_shape=jax.ShapeDtypeStruct((m, n), _F32),
        compiler_params=_params("parallel", "arbitrary"),
        name="proj_residual",
    )(x, w, res)


def _proj_swiglu(x, wg, wu, layer, *, bm=1024, bn=256):
    m, k = x.shape
    n = wg.shape[2]
    bm, bn = min(bm, m), min(bn, n)
    return pl.pallas_call(
        _swiglu_kernel,
        grid=(m // bm, n // bn),
        in_specs=[pl.BlockSpec((bm, k), lambda i, j: (i, 0)), _w_spec(k, bn, layer),
                  _w_spec(k, bn, layer)],
        out_specs=pl.BlockSpec((bm, bn), lambda i, j: (i, j)),
        out_shape=jax.ShapeDtypeStruct((m, n), _BF16),
        compiler_params=_params("parallel", "arbitrary"),
        name="proj_swiglu",
    )(x, wg, wu)


def _proj_qkv(x, w, layer, q_gain, k_gain, *, bm=1024, bn=1024):
    m, k = x.shape
    n = w.shape[2]
    third = n // 3
    bm, bn = min(bm, m), min(bn, third)
    per_third = third // bn
    gains = jnp.stack([q_gain, k_gain]).reshape(2, 1, HEAD_DIM)
    return pl.pallas_call(
        functools.partial(_qkv_kernel, n_norm_blocks=2 * per_third),
        grid=(m // bm, n // bn),
        in_specs=[pl.BlockSpec((bm, k), lambda i, j: (i, 0)), _w_spec(k, bn, layer),
                  pl.BlockSpec((1, 1, HEAD_DIM),
                               lambda i, j: (jnp.minimum(j // per_third, 1), 0, 0))],
        out_specs=pl.BlockSpec((bm, bn), lambda i, j: (i, j)),
        out_shape=jax.ShapeDtypeStruct((m, n), _BF16),
        compiler_params=_params("parallel", "arbitrary"),
        name="proj_qkv",
    )(x, w, gains)


def _spatial_kernel(u_ref, v_ref, vg_ref, w_ref, b_ref, o_ref):
    rows, width = v_ref.shape
    v = v_ref[...].astype(_F32)
    ms = jnp.mean(v * v, axis=-1, keepdims=True)
    vn = (v * lax.rsqrt(ms + RMS_EPS) * vg_ref[...]).astype(_BF16)
    t_chunk = lax.broadcasted_iota(jnp.int32, (GMLP_BLOCK, GMLP_BLOCK), 0) // CHUNK
    s_chunk = lax.broadcasted_iota(jnp.int32, (GMLP_BLOCK, GMLP_BLOCK), 1) // CHUNK
    causal = s_chunk <= t_chunk
    for g in range(width // HEAD_DIM):
        cols = slice(g * HEAD_DIM, (g + 1) * HEAD_DIM)
        w = jnp.where(causal, w_ref[g], 0.0).astype(_BF16)
        for p in range(rows // GMLP_BLOCK):
            rws = slice(p * GMLP_BLOCK, (p + 1) * GMLP_BLOCK)
            sv = _dot(w, vn[rws, cols]) + b_ref[:, cols]
            o_ref[rws, cols] = (u_ref[rws, cols].astype(_F32) * sv).astype(o_ref.dtype)


def _spatial_gate(z, v_gain, w_s, b_s, *, rows=256):
    m, two_w = z.shape
    width = two_w // 2
    rows = min(rows, m)
    groups = width // HEAD_DIM
    b_full = jnp.repeat(b_s.T, HEAD_DIM, axis=1)
    return pl.pallas_call(
        _spatial_kernel,
        grid=(m // rows,),
        in_specs=[pl.BlockSpec((rows, width), lambda i: (i, 0)),
                  pl.BlockSpec((rows, width), lambda i: (i, 1)),
                  pl.BlockSpec((1, width), lambda i: (0, 0)),
                  pl.BlockSpec((groups, GMLP_BLOCK, GMLP_BLOCK), lambda i: (0, 0, 0)),
                  pl.BlockSpec((GMLP_BLOCK, width), lambda i: (0, 0))],
        out_specs=pl.BlockSpec((rows, width), lambda i: (i, 0)),
        out_shape=jax.ShapeDtypeStruct((m, width), _BF16),
        compiler_params=_params("parallel"),
        name="gmlp_spatial",
    )(z, z, v_gain.reshape(1, width), w_s, b_full)


def _sb_attn_kernel(q_ref, k_ref, v_ref, cs_ref, o_ref, acc_ref, carry_ref, *, nsub, tb):
    base = pl.program_id(2) * nsub
    cs = cs_ref[...]
    groups = tb // LANE
    t_idx = lax.broadcasted_iota(jnp.int32, (tb, tb), 0)
    s_idx = lax.broadcasted_iota(jnp.int32, (tb, tb), 1)
    strict = s_idx < t_idx

    def block(r, kb, *, diagonal):
        q = q_ref[r * tb:(r + 1) * tb, :]
        k0 = pl.multiple_of(kb * tb, tb)
        k = k_ref[pl.ds(k0, tb), :]
        v = v_ref[pl.ds(k0, tb), :]
        z2 = lax.dot_general(q, k, (((1,), (1,)), ((), ())), preferred_element_type=_F32)
        sp = jnp.maximum(z2, 0.0) + jnp.log2(1.0 + jnp.exp2(-jnp.abs(z2)))
        log_beta = z2 - sp
        if diagonal:
            sp = jnp.where(strict, sp, 0.0)
            carry = None
        else:
            carry = carry_ref[r]
        a_parts = [None] * groups
        for g in reversed(range(groups)):
            cols = slice(g * LANE, (g + 1) * LANE)
            spg = sp[:, cols]
            hi = spg.astype(_BF16)
            lo = (spg - hi.astype(_F32)).astype(_BF16)
            sums = _dot(jnp.concatenate([hi, lo], axis=1), cs)
            log_a = log_beta[:, cols] - sums[:, :LANE]
            if carry is not None:
                log_a = log_a - carry
            a = jnp.exp2(log_a)
            if diagonal:
                a = jnp.where(strict[:, cols], a, 0.0)
            a_parts[g] = a.astype(_BF16)
            carry = sums[:, LANE:] if carry is None else carry + sums[:, LANE:]
        carry_ref[r] = carry
        pv = _dot(jnp.concatenate(a_parts, axis=1), v)
        if diagonal:
            acc_ref[r] = pv
        else:
            acc_ref[r] += pv

    def finished(r_from):
        return (jnp.min(carry_ref[r_from:]) >= EXIT_LOG2).astype(jnp.int32)

    for r in range(nsub):
        block(r, base + r, diagonal=True)

    def cond(state):
        i, done = state
        return jnp.logical_and(i <= base, done == 0)

    def body(state):
        i, _ = state
        for r in range(nsub):
            block(r, base + r - i, diagonal=False)
        return i + 1, finished(0)

    lax.while_loop(cond, body, (jnp.int32(1), finished(0)))

    for j in range(1, nsub):
        @pl.when(finished(j) == 0)
        def _():
            for r in range(j, nsub):
                block(r, r - j, diagonal=False)

    for r in range(nsub):
        o_ref[r * tb:(r + 1) * tb, :] = acc_ref[r].astype(o_ref.dtype)


def _suffix_sum_matrix():
    j = lax.broadcasted_iota(jnp.int32, (LANE, LANE), 0)
    s = lax.broadcasted_iota(jnp.int32, (LANE, LANE), 1)
    half = jnp.concatenate([(j > s).astype(_BF16), jnp.ones((LANE, LANE), _BF16)], axis=1)
    return jnp.concatenate([half, half], axis=0)


def _sb_attention(qkv, batch, seq, *, nsub=4, tb=256):
    m, three_hd = qkv.shape
    heads = three_hd // (3 * HEAD_DIM)
    tb = min(tb, seq)
    nsub = min(nsub, seq // tb)
    tq = nsub * tb
    nq = seq // tq
    return pl.pallas_call(
        functools.partial(_sb_attn_kernel, nsub=nsub, tb=tb),
        grid=(batch, heads, nq),
        in_specs=[pl.BlockSpec((tq, HEAD_DIM), lambda b, h, i: (b * nq + i, h)),
                  pl.BlockSpec((seq, HEAD_DIM), lambda b, h, i: (b, heads + h)),
                  pl.BlockSpec((seq, HEAD_DIM), lambda b, h, i: (b, 2 * heads + h)),
                  pl.BlockSpec((2 * LANE, 2 * LANE), lambda b, h, i: (0, 0))],
        out_specs=pl.BlockSpec((tq, HEAD_DIM), lambda b, h, i: (b * nq + i, h)),
        out_shape=jax.ShapeDtypeStruct((m, heads * HEAD_DIM), _BF16),
        scratch_shapes=[pltpu.VMEM((nsub, tb, HEAD_DIM), _F32), pltpu.VMEM((nsub, tb, LANE), _F32)],
        compiler_params=_params("parallel", "parallel", "arbitrary"),
        name="sb_attention",
    )(qkv, qkv, qkv, _suffix_sum_matrix())


def kernel(x, a_norm, a_w_in, a_v_norm, a_w_s, a_b_s, a_w_out, b_norm, b_w_qkv, b_q_norm, b_k_norm, b_w_out, f_norm, f_w_gate, f_w_up, f_w_down):
    batch, seq, d = x.shape
    depth = f_norm.shape[0]
    w_in, w_a_out = _to_bf16(a_w_in), _to_bf16(a_w_out)
    w_qkv, w_b_out = _to_bf16(b_w_qkv), _to_bf16(b_w_out)
    w_gate, w_up, w_down = _to_bf16(f_w_gate), _to_bf16(f_w_up), _to_bf16(f_w_down)
    q_scale = LOG2E / math.sqrt(HEAD_DIM)
    xr = x.reshape(batch * seq, d)
    for i in range(depth):
        j = i // 2
        if i % 2 == 0:
            h = _rmsnorm(xr, a_norm[j])
            z = _proj_gelu(h, w_in, j)
            gated = _spatial_gate(z, a_v_norm[j], a_w_s[j], a_b_s[j])
            xr = _proj_residual(gated, w_a_out, j, xr, bm=1024, bn=1024)
        else:
            h = _rmsnorm(xr, b_norm[j])
            qkv = _proj_qkv(h, w_qkv, j, b_q_norm[j] * q_scale, b_k_norm[j])
            o = _sb_attention(qkv, batch, seq)
            xr = _proj_residual(o, w_b_out, j, xr, bm=1024, bn=1024)
        h = _rmsnorm(xr, f_norm[i])
        a = _proj_swiglu(h, w_gate, w_up, i)
        xr = _proj_residual(a, w_down, i, xr, bm=512, bn=512)
    return xr.reshape(batch, seq, d)
```

```python
import functools
import math

import jax
import jax.numpy as jnp
from jax import lax
from jax.experimental import pallas as pl
from jax.experimental.pallas import tpu as pltpu

RMS_EPS = 1e-6
CHUNK = 64
GMLP_BLOCK = 128
HEAD_DIM = 128
LANE = 128
MXU_COLS = 256
BF16_SUBLANES = 16
V7X_VMEM_BYTES = 64 * 1024 * 1024
VMEM_LIMIT_BYTES = V7X_VMEM_BYTES - 8 * 1024 * 1024
LOG2E = 1.4426950408889634
EXIT_LOG2 = 151.0

BLOCK_GELU = (1024, 1024)
BLOCK_QKV = (1024, 1024)
BLOCK_OUT = (1024, 512)
BLOCK_SWIGLU = (1024, 256)
BLOCK_DOWN = (512, 512)

_BF16 = jnp.bfloat16
_F32 = jnp.float32


def _params(*semantics):
    return pltpu.CompilerParams(dimension_semantics=semantics,
                                vmem_limit_bytes=VMEM_LIMIT_BYTES)


def _dot(a, b):
    return jnp.dot(a, b, preferred_element_type=_F32)


def _lane_partial_sums(sq):
    part = sq[:, :LANE]
    for c in range(1, sq.shape[1] // LANE):
        part = part + sq[:, c * LANE:(c + 1) * LANE]
    return part


def _row_rsqrt(ss_ref, width):
    return lax.rsqrt(jnp.sum(ss_ref[...], axis=-1, keepdims=True) * (1.0 / width) + RMS_EPS)


def _cast_kernel(w_ref, o_ref):
    o_ref[...] = w_ref[...].astype(o_ref.dtype)


def _to_bf16(w, *, block_bytes=4 * 1024 * 1024):
    layers, k, n = w.shape
    bk = BF16_SUBLANES
    while 2 * bk * n * 4 <= block_bytes and k % (2 * bk) == 0:
        bk *= 2
    assert k % bk == 0, (k, bk)
    return pl.pallas_call(
        _cast_kernel,
        grid=(layers, k // bk),
        in_specs=[pl.BlockSpec((None, bk, n), lambda l, i: (l, i, 0))],
        out_specs=pl.BlockSpec((None, bk, n), lambda l, i: (l, i, 0)),
        out_shape=jax.ShapeDtypeStruct(w.shape, _BF16),
        compiler_params=_params("parallel", "parallel"),
        name="cast_bf16",
    )(w)


def _norm_inputs_kernel(x_ref, g_ref, xg_ref, ss_ref):
    x = x_ref[...]
    xg_ref[...] = (x * g_ref[...]).astype(xg_ref.dtype)
    ss_ref[...] = _lane_partial_sums(x * x)


def _norm_inputs(x, gain, *, rows=512):
    m, d = x.shape
    rows = min(rows, m)
    return pl.pallas_call(
        _norm_inputs_kernel,
        grid=(m // rows,),
        in_specs=[pl.BlockSpec((rows, d), lambda i: (i, 0)),
                  pl.BlockSpec((1, d), lambda i: (0, 0))],
        out_specs=[pl.BlockSpec((rows, d), lambda i: (i, 0)),
                   pl.BlockSpec((rows, LANE), lambda i: (i, 0))],
        out_shape=[jax.ShapeDtypeStruct((m, d), _BF16), jax.ShapeDtypeStruct((m, LANE), _F32)],
        compiler_params=_params("parallel"),
        name="norm_inputs",
    )(x, gain.reshape(1, d))


def _w_spec(k, bn, layer):
    return pl.BlockSpec((None, k, bn), lambda i, j: (layer, 0, j))


def _row_spec(bm, width):
    return pl.BlockSpec((bm, width), lambda i, j: (i, 0))


def _tile_spec(bm, bn):
    return pl.BlockSpec((bm, bn), lambda i, j: (i, j))


def _gelu_kernel(xg_ref, ss_ref, w_ref, o_ref):
    acc = _dot(xg_ref[...], w_ref[...]) * _row_rsqrt(ss_ref, xg_ref.shape[1])
    o_ref[...] = (0.5 * acc * (1.0 + lax.erf(acc * math.sqrt(0.5)))).astype(o_ref.dtype)


def _swiglu_kernel(xg_ref, ss_ref, wg_ref, wu_ref, o_ref):
    xg = xg_ref[...]
    r = _row_rsqrt(ss_ref, xg_ref.shape[1])
    g = _dot(xg, wg_ref[...].astype(_BF16)) * r
    u = _dot(xg, wu_ref[...].astype(_BF16)) * r
    o_ref[...] = (g * jax.nn.sigmoid(g) * u).astype(o_ref.dtype)


def _qkv_kernel(xg_ref, ss_ref, w_ref, g_ref, o_ref, *, n_norm_blocks):
    normed_block = pl.program_id(1) < n_norm_blocks
    gain = g_ref[0]
    xg = xg_ref[...]
    r = _row_rsqrt(ss_ref, xg_ref.shape[1])
    for c in range(o_ref.shape[1] // MXU_COLS):
        acc = _dot(xg, w_ref[:, c * MXU_COLS:(c + 1) * MXU_COLS]) * r
        for h in range(MXU_COLS // HEAD_DIM):
            blk = acc[:, h * HEAD_DIM:(h + 1) * HEAD_DIM]
            ms = jnp.mean(blk * blk, axis=-1, keepdims=True)
            normed = blk * lax.rsqrt(ms + RMS_EPS) * gain
            lo = c * MXU_COLS + h * HEAD_DIM
            o_ref[:, lo:lo + HEAD_DIM] = jnp.where(normed_block, normed, blk).astype(o_ref.dtype)


def _residual_kernel(x_ref, w_ref, r_ref, o_ref):
    o_ref[...] = r_ref[...] + _dot(x_ref[...], w_ref[...])


def _residual_norm_kernel(x_ref, w_ref, r_ref, g_ref, o_ref, xg_ref, ss_ref):
    new = r_ref[...] + _dot(x_ref[...], w_ref[...])
    o_ref[...] = new
    xg_ref[...] = (new * g_ref[...]).astype(xg_ref.dtype)
    ss_ref[...] = _lane_partial_sums(new * new)


def _proj_gelu(xg, ss, w, layer):
    m, k = xg.shape
    n = w.shape[2]
    bm, bn = min(BLOCK_GELU[0], m), min(BLOCK_GELU[1], n)
    return pl.pallas_call(
        _gelu_kernel,
        grid=(m // bm, n // bn),
        in_specs=[_row_spec(bm, k), _row_spec(bm, ss.shape[1]), _w_spec(k, bn, layer)],
        out_specs=_tile_spec(bm, bn),
        out_shape=jax.ShapeDtypeStruct((m, n), _BF16),
        compiler_params=_params("parallel", "arbitrary"),
        name="proj_gelu",
    )(xg, ss, w)


def _proj_swiglu(xg, ss, wg, wu, layer):
    m, k = xg.shape
    n = wg.shape[2]
    bm, bn = min(BLOCK_SWIGLU[0], m), min(BLOCK_SWIGLU[1], n)
    return pl.pallas_call(
        _swiglu_kernel,
        grid=(m // bm, n // bn),
        in_specs=[_row_spec(bm, k), _row_spec(bm, ss.shape[1]), _w_spec(k, bn, layer), _w_spec(k, bn, layer)],
        out_specs=_tile_spec(bm, bn),
        out_shape=jax.ShapeDtypeStruct((m, n), _BF16),
        compiler_params=_params("parallel", "arbitrary"),
        name="proj_swiglu",
    )(xg, ss, wg, wu)


def _proj_qkv(xg, ss, w, layer, q_gain, k_gain):
    m, k = xg.shape
    n = w.shape[2]
    third = n // 3
    bm, bn = min(BLOCK_QKV[0], m), min(BLOCK_QKV[1], third)
    per_third = third // bn
    gains = jnp.stack([q_gain, k_gain]).reshape(2, 1, HEAD_DIM)
    return pl.pallas_call(
        functools.partial(_qkv_kernel, n_norm_blocks=2 * per_third),
        grid=(m // bm, n // bn),
        in_specs=[_row_spec(bm, k), _row_spec(bm, ss.shape[1]), _w_spec(k, bn, layer),
                  pl.BlockSpec((1, 1, HEAD_DIM),
                               lambda i, j: (jnp.minimum(j // per_third, 1), 0, 0))],
        out_specs=_tile_spec(bm, bn),
        out_shape=jax.ShapeDtypeStruct((m, n), _BF16),
        compiler_params=_params("parallel", "arbitrary"),
        name="proj_qkv",
    )(xg, ss, w, gains)


def _proj_residual(x, w, layer, res, next_gain, block):
    m, k = x.shape
    n = w.shape[2]
    bm, bn = min(block[0], m), min(block[1], n)
    in_specs = [_row_spec(bm, k), _w_spec(k, bn, layer), _tile_spec(bm, bn)]
    if next_gain is None:
        return pl.pallas_call(
            _residual_kernel,
            grid=(m // bm, n // bn),
            in_specs=in_specs,
            out_specs=_tile_spec(bm, bn),
            out_shape=jax.ShapeDtypeStruct((m, n), _F32),
            compiler_params=_params("parallel", "arbitrary"),
            name="proj_residual",
        )(x, w, res)
    return pl.pallas_call(
        _residual_norm_kernel,
        grid=(m // bm, n // bn),
        in_specs=in_specs + [pl.BlockSpec((1, bn), lambda i, j: (0, j))],
        out_specs=[_tile_spec(bm, bn), _tile_spec(bm, bn), _tile_spec(bm, LANE)],
        out_shape=[jax.ShapeDtypeStruct((m, n), _F32), jax.ShapeDtypeStruct((m, n), _BF16),
                   jax.ShapeDtypeStruct((m, (n // bn) * LANE), _F32)],
        compiler_params=_params("parallel", "arbitrary"),
        name="proj_residual_norm",
    )(x, w, res, next_gain.reshape(1, n))


def _spatial_kernel(u_ref, v_ref, vg_ref, w_ref, b_ref, o_ref):
    rows, width = v_ref.shape
    v = v_ref[...].astype(_F32)
    ms = jnp.mean(v * v, axis=-1, keepdims=True)
    vn = (v * lax.rsqrt(ms + RMS_EPS) * vg_ref[...]).astype(_BF16)
    t_chunk = lax.broadcasted_iota(jnp.int32, (GMLP_BLOCK, GMLP_BLOCK), 0) // CHUNK
    s_chunk = lax.broadcasted_iota(jnp.int32, (GMLP_BLOCK, GMLP_BLOCK), 1) // CHUNK
    causal = s_chunk <= t_chunk
    for g in range(width // HEAD_DIM):
        cols = slice(g * HEAD_DIM, (g + 1) * HEAD_DIM)
        w = jnp.where(causal, w_ref[g], 0.0).astype(_BF16)
        for p in range(rows // GMLP_BLOCK):
            rws = slice(p * GMLP_BLOCK, (p + 1) * GMLP_BLOCK)
            sv = _dot(w, vn[rws, cols]) + b_ref[:, cols]
            o_ref[rws, cols] = (u_ref[rws, cols].astype(_F32) * sv).astype(o_ref.dtype)


def _spatial_gate(z, v_gain, w_s, b_s, *, rows=256):
    m, two_w = z.shape
    width = two_w // 2
    rows = min(rows, m)
    groups = width // HEAD_DIM
    b_full = jnp.repeat(b_s.T, HEAD_DIM, axis=1)
    return pl.pallas_call(
        _spatial_kernel,
        grid=(m // rows,),
        in_specs=[pl.BlockSpec((rows, width), lambda i: (i, 0)),
                  pl.BlockSpec((rows, width), lambda i: (i, 1)),
                  pl.BlockSpec((1, width), lambda i: (0, 0)),
                  pl.BlockSpec((groups, GMLP_BLOCK, GMLP_BLOCK), lambda i: (0, 0, 0)),
                  pl.BlockSpec((GMLP_BLOCK, width), lambda i: (0, 0))],
        out_specs=pl.BlockSpec((rows, width), lambda i: (i, 0)),
        out_shape=jax.ShapeDtypeStruct((m, width), _BF16),
        compiler_params=_params("parallel"),
        name="gmlp_spatial",
    )(z, z, v_gain.reshape(1, width), w_s, b_full)


def _sb_attn_kernel(q_ref, k_ref, v_ref, cs_ref, o_ref, acc_ref, carry_ref, *, nsub, tb):
    base = pl.program_id(2) * nsub
    cs = cs_ref[...]
    groups = tb // LANE
    t_idx = lax.broadcasted_iota(jnp.int32, (tb, tb), 0)
    s_idx = lax.broadcasted_iota(jnp.int32, (tb, tb), 1)
    strict = s_idx < t_idx

    def sweep(blocks):
        staged, operands = [], []
        for r, kb, diagonal, valid in blocks:
            q = q_ref[r * tb:(r + 1) * tb, :]
            k0 = pl.multiple_of(kb * tb, tb)
            k = k_ref[pl.ds(k0, tb), :]
            z2 = lax.dot_general(q, k, (((1,), (1,)), ((), ())), preferred_element_type=_F32)
            sp = jnp.maximum(z2, 0.0) + jnp.log2(1.0 + jnp.exp2(-jnp.abs(z2)))
            log_beta = z2 - sp
            if diagonal:
                sp = jnp.where(strict, sp, 0.0)
            if valid is not None:
                sp = jnp.where(valid, sp, 0.0)
            for g in range(groups):
                spg = sp[:, g * LANE:(g + 1) * LANE]
                hi = spg.astype(_BF16)
                lo = (spg - hi.astype(_F32)).astype(_BF16)
                operands.append(jnp.concatenate([hi, lo], axis=1))
            staged.append((r, k0, diagonal, valid, log_beta))
        sums = _dot(jnp.concatenate(operands, axis=0), cs)
        for n, (r, k0, diagonal, valid, log_beta) in enumerate(staged):
            carry = None if diagonal else carry_ref[r]
            a_parts = [None] * groups
            for g in reversed(range(groups)):
                sg = sums[(n * groups + g) * tb:(n * groups + g + 1) * tb, :]
                log_a = log_beta[:, g * LANE:(g + 1) * LANE] - sg[:, :LANE]
                if carry is not None:
                    log_a = log_a - carry
                a = jnp.exp2(log_a)
                if diagonal:
                    a = jnp.where(strict[:, g * LANE:(g + 1) * LANE], a, 0.0)
                if valid is not None:
                    a = jnp.where(valid, a, 0.0)
                a_parts[g] = a.astype(_BF16)
                carry = sg[:, LANE:] if carry is None else carry + sg[:, LANE:]
            carry_ref[r] = carry
            pv = _dot(jnp.concatenate(a_parts, axis=1), v_ref[pl.ds(k0, tb), :])
            if diagonal:
                acc_ref[r] = pv
            else:
                acc_ref[r] += pv

    def finished(r_from):
        return (jnp.min(carry_ref[r_from:]) >= EXIT_LOG2).astype(jnp.int32)

    sweep([(r, base + r, True, None) for r in range(nsub)]
          + [(0, jnp.maximum(base - 1, 0), False, base > 0)]
          + [(r, base + r - 1, False, None) for r in range(1, nsub)])

    def cond(state):
        i, done = state
        return jnp.logical_and(i <= base, done == 0)

    def body(state):
        i, _ = state
        sweep([(r, base + r - i, False, None) for r in range(nsub)])
        return i + 1, finished(0)

    lax.while_loop(cond, body, (jnp.int32(2), finished(0)))

    for j in range(1, nsub):
        todo = finished(j) == 0
        if j == 1:
            todo = jnp.logical_and(todo, base > 0)

        @pl.when(todo)
        def _():
            sweep([(r, r - j, False, None) for r in range(j, nsub)])

    for r in range(nsub):
        o_ref[r * tb:(r + 1) * tb, :] = acc_ref[r].astype(o_ref.dtype)


def _suffix_sum_matrix():
    j = lax.broadcasted_iota(jnp.int32, (LANE, LANE), 0)
    s = lax.broadcasted_iota(jnp.int32, (LANE, LANE), 1)
    half = jnp.concatenate([(j > s).astype(_BF16), jnp.ones((LANE, LANE), _BF16)], axis=1)
    return jnp.concatenate([half, half], axis=0)


def _sb_attention(qkv, batch, seq, *, nsub=4, tb=256):
    m, three_hd = qkv.shape
    heads = three_hd // (3 * HEAD_DIM)
    tb = min(tb, seq)
    nsub = min(nsub, seq // tb)
    tq = nsub * tb
    nq = seq // tq
    return pl.pallas_call(
        functools.partial(_sb_attn_kernel, nsub=nsub, tb=tb),
        grid=(batch, heads, nq),
        in_specs=[pl.BlockSpec((tq, HEAD_DIM), lambda b, h, i: (b * nq + i, h)),
                  pl.BlockSpec((seq, HEAD_DIM), lambda b, h, i: (b, heads + h)),
                  pl.BlockSpec((seq, HEAD_DIM), lambda b, h, i: (b, 2 * heads + h)),
                  pl.BlockSpec((2 * LANE, 2 * LANE), lambda b, h, i: (0, 0))],
        out_specs=pl.BlockSpec((tq, HEAD_DIM), lambda b, h, i: (b * nq + i, h)),
        out_shape=jax.ShapeDtypeStruct((m, heads * HEAD_DIM), _BF16),
        scratch_shapes=[pltpu.VMEM((nsub, tb, HEAD_DIM), _F32), pltpu.VMEM((nsub, tb, LANE), _F32)],
        compiler_params=_params("parallel", "parallel", "arbitrary"),
        name="sb_attention",
    )(qkv, qkv, qkv, _suffix_sum_matrix())


def kernel(x, a_norm, a_w_in, a_v_norm, a_w_s, a_b_s, a_w_out, b_norm, b_w_qkv, b_q_norm, b_k_norm, b_w_out, f_norm, f_w_gate, f_w_up, f_w_down):
    batch, seq, d = x.shape
    depth = f_norm.shape[0]
    w_in, w_a_out = _to_bf16(a_w_in), _to_bf16(a_w_out)
    w_qkv, w_b_out = _to_bf16(b_w_qkv), _to_bf16(b_w_out)
    w_down = _to_bf16(f_w_down)
    q_scale = LOG2E / math.sqrt(HEAD_DIM)

    def mixer_gain(i):
        return a_norm[i // 2] if i % 2 == 0 else b_norm[i // 2]

    xr = x.reshape(batch * seq, d)
    xg, ss = _norm_inputs(xr, mixer_gain(0))
    for i in range(depth):
        j = i // 2
        if i % 2 == 0:
            z = _proj_gelu(xg, ss, w_in, j)
            mixed = _spatial_gate(z, a_v_norm[j], a_w_s[j], a_b_s[j])
            xr, xg, ss = _proj_residual(mixed, w_a_out, j, xr, f_norm[i], BLOCK_OUT)
        else:
            qkv = _proj_qkv(xg, ss, w_qkv, j, b_q_norm[j] * q_scale, b_k_norm[j])
            mixed = _sb_attention(qkv, batch, seq)
            xr, xg, ss = _proj_residual(mixed, w_b_out, j, xr, f_norm[i], BLOCK_OUT)
        a = _proj_swiglu(xg, ss, f_w_gate, f_w_up, i)
        if i + 1 < depth:
            xr, xg, ss = _proj_residual(a, w_down, i, xr, mixer_gain(i + 1), BLOCK_DOWN)
        else:
            xr = _proj_residual(a, w_down, i, xr, None, BLOCK_DOWN)
    return xr.reshape(batch, seq, d)
```

```python
import functools
import math

import jax
import jax.numpy as jnp
from jax import lax
from jax.experimental import pallas as pl
from jax.experimental.pallas import tpu as pltpu

RMS_EPS = 1e-6
CHUNK = 64
GMLP_BLOCK = 128
HEAD_DIM = 128
LANE = 128
MXU_COLS = 256
BF16_SUBLANES = 16
V7X_VMEM_BYTES = 64 * 1024 * 1024
VMEM_LIMIT_BYTES = V7X_VMEM_BYTES - 8 * 1024 * 1024
LOG2E = 1.4426950408889634
EXIT_LOG2 = 151.0

BLOCK_GELU = (1024, 1024)
BLOCK_QKV = (1024, 1024)
BLOCK_OUT = (1024, 512)
BLOCK_SWIGLU = (1024, 256)
BLOCK_DOWN = (512, 512)

_BF16 = jnp.bfloat16
_F32 = jnp.float32


def _params(*semantics):
    return pltpu.CompilerParams(dimension_semantics=semantics,
                                vmem_limit_bytes=VMEM_LIMIT_BYTES)


def _dot(a, b):
    return jnp.dot(a, b, preferred_element_type=_F32)


def _lane_partial_sums(sq):
    part = sq[:, :LANE]
    for c in range(1, sq.shape[1] // LANE):
        part = part + sq[:, c * LANE:(c + 1) * LANE]
    return part


def _row_rsqrt(ss_ref, width):
    return lax.rsqrt(jnp.sum(ss_ref[...], axis=-1, keepdims=True) * (1.0 / width) + RMS_EPS)


def _cast_kernel(w_ref, o_ref):
    o_ref[...] = w_ref[...].astype(o_ref.dtype)


def _to_bf16(w, *, block_bytes=4 * 1024 * 1024):
    layers, k, n = w.shape
    bk = BF16_SUBLANES
    while 2 * bk * n * 4 <= block_bytes and k % (2 * bk) == 0:
        bk *= 2
    assert k % bk == 0, (k, bk)
    return pl.pallas_call(
        _cast_kernel,
        grid=(layers, k // bk),
        in_specs=[pl.BlockSpec((None, bk, n), lambda l, i: (l, i, 0))],
        out_specs=pl.BlockSpec((None, bk, n), lambda l, i: (l, i, 0)),
        out_shape=jax.ShapeDtypeStruct(w.shape, _BF16),
        compiler_params=_params("parallel", "parallel"),
        name="cast_bf16",
    )(w)


def _norm_inputs_kernel(x_ref, g_ref, xg_ref, ss_ref):
    x = x_ref[...]
    xg_ref[...] = (x * g_ref[...]).astype(xg_ref.dtype)
    ss_ref[...] = _lane_partial_sums(x * x)


def _norm_inputs(x, gain, *, rows=512):
    m, d = x.shape
    rows = min(rows, m)
    return pl.pallas_call(
        _norm_inputs_kernel,
        grid=(m // rows,),
        in_specs=[pl.BlockSpec((rows, d), lambda i: (i, 0)),
                  pl.BlockSpec((1, d), lambda i: (0, 0))],
        out_specs=[pl.BlockSpec((rows, d), lambda i: (i, 0)),
                   pl.BlockSpec((rows, LANE), lambda i: (i, 0))],
        out_shape=[jax.ShapeDtypeStruct((m, d), _BF16), jax.ShapeDtypeStruct((m, LANE), _F32)],
        compiler_params=_params("parallel"),
        name="norm_inputs",
    )(x, gain.reshape(1, d))


def _w_spec(k, bn, layer):
    return pl.BlockSpec((None, k, bn), lambda i, j: (layer, 0, j))


def _row_spec(bm, width):
    return pl.BlockSpec((bm, width), lambda i, j: (i, 0))


def _tile_spec(bm, bn):
    return pl.BlockSpec((bm, bn), lambda i, j: (i, j))


def _gelu_kernel(xg_ref, ss_ref, w_ref, o_ref):
    acc = _dot(xg_ref[...], w_ref[...]) * _row_rsqrt(ss_ref, xg_ref.shape[1])
    o_ref[...] = (0.5 * acc * (1.0 + lax.erf(acc * math.sqrt(0.5)))).astype(o_ref.dtype)


def _swiglu_kernel(xg_ref, ss_ref, wg_ref, wu_ref, o_ref):
    xg = xg_ref[...]
    r = _row_rsqrt(ss_ref, xg_ref.shape[1])
    g = _dot(xg, wg_ref[...].astype(_BF16)) * r
    u = _dot(xg, wu_ref[...].astype(_BF16)) * r
    o_ref[...] = (g * jax.nn.sigmoid(g) * u).astype(o_ref.dtype)


def _qkv_kernel(xg_ref, ss_ref, w_ref, g_ref, o_ref, *, n_norm_blocks):
    normed_block = pl.program_id(1) < n_norm_blocks
    gain = g_ref[0]
    xg = xg_ref[...]
    r = _row_rsqrt(ss_ref, xg_ref.shape[1])
    for c in range(o_ref.shape[1] // MXU_COLS):
        acc = _dot(xg, w_ref[:, c * MXU_COLS:(c + 1) * MXU_COLS]) * r
        for h in range(MXU_COLS // HEAD_DIM):
            blk = acc[:, h * HEAD_DIM:(h + 1) * HEAD_DIM]
            ms = jnp.mean(blk * blk, axis=-1, keepdims=True)
            normed = blk * lax.rsqrt(ms + RMS_EPS) * gain
            lo = c * MXU_COLS + h * HEAD_DIM
            o_ref[:, lo:lo + HEAD_DIM] = jnp.where(normed_block, normed, blk).astype(o_ref.dtype)


def _residual_kernel(x_ref, w_ref, r_ref, o_ref):
    o_ref[...] = r_ref[...] + _dot(x_ref[...], w_ref[...])


def _residual_norm_kernel(x_ref, w_ref, r_ref, g_ref, o_ref, xg_ref, ss_ref):
    new = r_ref[...] + _dot(x_ref[...], w_ref[...])
    o_ref[...] = new
    xg_ref[...] = (new * g_ref[...]).astype(xg_ref.dtype)
    ss_ref[...] = _lane_partial_sums(new * new)


def _proj_gelu(xg, ss, w, layer):
    m, k = xg.shape
    n = w.shape[2]
    bm, bn = min(BLOCK_GELU[0], m), min(BLOCK_GELU[1], n)
    return pl.pallas_call(
        _gelu_kernel,
        grid=(m // bm, n // bn),
        in_specs=[_row_spec(bm, k), _row_spec(bm, ss.shape[1]), _w_spec(k, bn, layer)],
        out_specs=_tile_spec(bm, bn),
        out_shape=jax.ShapeDtypeStruct((m, n), _BF16),
        compiler_params=_params("parallel", "arbitrary"),
        name="proj_gelu",
    )(xg, ss, w)


def _proj_swiglu(xg, ss, wg, wu, layer):
    m, k = xg.shape
    n = wg.shape[2]
    bm, bn = min(BLOCK_SWIGLU[0], m), min(BLOCK_SWIGLU[1], n)
    return pl.pallas_call(
        _swiglu_kernel,
        grid=(m // bm, n // bn),
        in_specs=[_row_spec(bm, k), _row_spec(bm, ss.shape[1]), _w_spec(k, bn, layer), _w_spec(k, bn, layer)],
        out_specs=_tile_spec(bm, bn),
        out_shape=jax.ShapeDtypeStruct((m, n), _BF16),
        compiler_params=_params("parallel", "arbitrary"),
        name="proj_swiglu",
    )(xg, ss, wg, wu)


def _proj_qkv(xg, ss, w, layer, q_gain, k_gain):
    m, k = xg.shape
    n = w.shape[2]
    third = n // 3
    bm, bn = min(BLOCK_QKV[0], m), min(BLOCK_QKV[1], third)
    per_third = third // bn
    gains = jnp.stack([q_gain, k_gain]).reshape(2, 1, HEAD_DIM)
    return pl.pallas_call(
        functools.partial(_qkv_kernel, n_norm_blocks=2 * per_third),
        grid=(m // bm, n // bn),
        in_specs=[_row_spec(bm, k), _row_spec(bm, ss.shape[1]), _w_spec(k, bn, layer),
                  pl.BlockSpec((1, 1, HEAD_DIM),
                               lambda i, j: (jnp.minimum(j // per_third, 1), 0, 0))],
        out_specs=_tile_spec(bm, bn),
        out_shape=jax.ShapeDtypeStruct((m, n), _BF16),
        compiler_params=_params("parallel", "arbitrary"),
        name="proj_qkv",
    )(xg, ss, w, gains)


def _proj_residual(x, w, layer, res, next_gain, block):
    m, k = x.shape
    n = w.shape[2]
    bm, bn = min(block[0], m), min(block[1], n)
    in_specs = [_row_spec(bm, k), _w_spec(k, bn, layer), _tile_spec(bm, bn)]
    if next_gain is None:
        return pl.pallas_call(
            _residual_kernel,
            grid=(m // bm, n // bn),
            in_specs=in_specs,
            out_specs=_tile_spec(bm, bn),
            out_shape=jax.ShapeDtypeStruct((m, n), _F32),
            compiler_params=_params("parallel", "arbitrary"),
            name="proj_residual",
        )(x, w, res)
    return pl.pallas_call(
        _residual_norm_kernel,
        grid=(m // bm, n // bn),
        in_specs=in_specs + [pl.BlockSpec((1, bn), lambda i, j: (0, j))],
        out_specs=[_tile_spec(bm, bn), _tile_spec(bm, bn), _tile_spec(bm, LANE)],
        out_shape=[jax.ShapeDtypeStruct((m, n), _F32), jax.ShapeDtypeStruct((m, n), _BF16),
                   jax.ShapeDtypeStruct((m, (n // bn) * LANE), _F32)],
        compiler_params=_params("parallel", "arbitrary"),
        name="proj_residual_norm",
    )(x, w, res, next_gain.reshape(1, n))


def _spatial_kernel(u_ref, v_ref, vg_ref, w_ref, b_ref, o_ref):
    rows, width = v_ref.shape
    v = v_ref[...].astype(_F32)
    ms = jnp.mean(v * v, axis=-1, keepdims=True)
    vn = (v * lax.rsqrt(ms + RMS_EPS) * vg_ref[...]).astype(_BF16)
    t_chunk = lax.broadcasted_iota(jnp.int32, (GMLP_BLOCK, GMLP_BLOCK), 0) // CHUNK
    s_chunk = lax.broadcasted_iota(jnp.int32, (GMLP_BLOCK, GMLP_BLOCK), 1) // CHUNK
    causal = s_chunk <= t_chunk
    for g in range(width // HEAD_DIM):
        cols = slice(g * HEAD_DIM, (g + 1) * HEAD_DIM)
        w = jnp.where(causal, w_ref[g], 0.0).astype(_BF16)
        for p in range(rows // GMLP_BLOCK):
            rws = slice(p * GMLP_BLOCK, (p + 1) * GMLP_BLOCK)
            sv = _dot(w, vn[rws, cols]) + b_ref[:, cols]
            o_ref[rws, cols] = (u_ref[rws, cols].astype(_F32) * sv).astype(o_ref.dtype)


def _spatial_gate(z, v_gain, w_s, b_s, *, rows=256):
    m, two_w = z.shape
    width = two_w // 2
    rows = min(rows, m)
    groups = width // HEAD_DIM
    b_full = jnp.repeat(b_s.T, HEAD_DIM, axis=1)
    return pl.pallas_call(
        _spatial_kernel,
        grid=(m // rows,),
        in_specs=[pl.BlockSpec((rows, width), lambda i: (i, 0)),
                  pl.BlockSpec((rows, width), lambda i: (i, 1)),
                  pl.BlockSpec((1, width), lambda i: (0, 0)),
                  pl.BlockSpec((groups, GMLP_BLOCK, GMLP_BLOCK), lambda i: (0, 0, 0)),
                  pl.BlockSpec((GMLP_BLOCK, width), lambda i: (0, 0))],
        out_specs=pl.BlockSpec((rows, width), lambda i: (i, 0)),
        out_shape=jax.ShapeDtypeStruct((m, width), _BF16),
        compiler_params=_params("parallel"),
        name="gmlp_spatial",
    )(z, z, v_gain.reshape(1, width), w_s, b_full)


def _sb_attn_kernel(q_ref, k_ref, v_ref, cs_ref, o_ref, acc_ref, carry_ref, *, nsub, tb):
    base = pl.program_id(2) * nsub
    cs = cs_ref[...]
    groups = tb // LANE
    t_idx = lax.broadcasted_iota(jnp.int32, (tb, tb), 0)
    s_idx = lax.broadcasted_iota(jnp.int32, (tb, tb), 1)
    strict = s_idx < t_idx

    def sweep(blocks):
        staged, operands = [], []
        for r, kb, diagonal, valid in blocks:
            q = q_ref[r * tb:(r + 1) * tb, :]
            k0 = pl.multiple_of(kb * tb, tb)
            k = k_ref[pl.ds(k0, tb), :]
            z2 = lax.dot_general(q, k, (((1,), (1,)), ((), ())), preferred_element_type=_F32)
            sp = jnp.maximum(z2, 0.0) + jnp.log2(1.0 + jnp.exp2(-jnp.abs(z2)))
            log_beta = z2 - sp
            if diagonal:
                sp = jnp.where(strict, sp, 0.0)
            if valid is not None:
                sp = jnp.where(valid, sp, 0.0)
            for g in range(groups):
                spg = sp[:, g * LANE:(g + 1) * LANE]
                hi = spg.astype(_BF16)
                lo = (spg - hi.astype(_F32)).astype(_BF16)
                operands.append(jnp.concatenate([hi, lo], axis=1))
            staged.append((r, k0, diagonal, valid, log_beta))
        sums = _dot(jnp.concatenate(operands, axis=0), cs)
        for n, (r, k0, diagonal, valid, log_beta) in enumerate(staged):
            carry = None if diagonal else carry_ref[r]
            a_parts = [None] * groups
            for g in reversed(range(groups)):
                sg = sums[(n * groups + g) * tb:(n * groups + g + 1) * tb, :]
                log_a = log_beta[:, g * LANE:(g + 1) * LANE] - sg[:, :LANE]
                if carry is not None:
                    log_a = log_a - carry
                a = jnp.exp2(log_a)
                if diagonal:
                    a = jnp.where(strict[:, g * LANE:(g + 1) * LANE], a, 0.0)
                if valid is not None:
                    a = jnp.where(valid, a, 0.0)
                a_parts[g] = a.astype(_BF16)
                carry = sg[:, LANE:] if carry is None else carry + sg[:, LANE:]
            carry_ref[r] = carry
            pv = _dot(jnp.concatenate(a_parts, axis=1), v_ref[pl.ds(k0, tb), :])
            if diagonal:
                acc_ref[r] = pv
            else:
                acc_ref[r] += pv

    def finished(r_from):
        return (jnp.min(carry_ref[r_from:]) >= EXIT_LOG2).astype(jnp.int32)

    sweep([(r, base + r, True, None) for r in range(nsub)]
          + [(0, jnp.maximum(base - 1, 0), False, base > 0)]
          + [(r, base + r - 1, False, None) for r in range(1, nsub)])

    def cond(state):
        i, done = state
        return jnp.logical_and(i <= base, done == 0)

    def body(state):
        i, _ = state
        sweep([(r, base + r - i, False, None) for r in range(nsub)])
        return i + 1, finished(0)

    _, done = lax.while_loop(cond, body, (jnp.int32(2), finished(0)))

    @pl.when(done == 0)
    def _():
        for j in range(1, nsub):
            todo = finished(j) == 0
            if j == 1:
                todo = jnp.logical_and(todo, base > 0)

            @pl.when(todo)
            def _():
                sweep([(r, r - j, False, None) for r in range(j, nsub)])

    for r in range(nsub):
        o_ref[r * tb:(r + 1) * tb, :] = acc_ref[r].astype(o_ref.dtype)


def _suffix_sum_matrix():
    j = lax.broadcasted_iota(jnp.int32, (LANE, LANE), 0)
    s = lax.broadcasted_iota(jnp.int32, (LANE, LANE), 1)
    half = jnp.concatenate([(j > s).astype(_BF16), jnp.ones((LANE, LANE), _BF16)], axis=1)
    return jnp.concatenate([half, half], axis=0)


def _sb_attention(qkv, batch, seq, *, nsub=8, tb=256):
    m, three_hd = qkv.shape
    heads = three_hd // (3 * HEAD_DIM)
    tb = min(tb, seq)
    nsub = min(nsub, seq // tb)
    tq = nsub * tb
    nq = seq // tq
    return pl.pallas_call(
        functools.partial(_sb_attn_kernel, nsub=nsub, tb=tb),
        grid=(batch, heads, nq),
        in_specs=[pl.BlockSpec((tq, HEAD_DIM), lambda b, h, i: (b * nq + i, h)),
                  pl.BlockSpec((seq, HEAD_DIM), lambda b, h, i: (b, heads + h)),
                  pl.BlockSpec((seq, HEAD_DIM), lambda b, h, i: (b, 2 * heads + h)),
                  pl.BlockSpec((2 * LANE, 2 * LANE), lambda b, h, i: (0, 0))],
        out_specs=pl.BlockSpec((tq, HEAD_DIM), lambda b, h, i: (b * nq + i, h)),
        out_shape=jax.ShapeDtypeStruct((m, heads * HEAD_DIM), _BF16),
        scratch_shapes=[pltpu.VMEM((nsub, tb, HEAD_DIM), _F32), pltpu.VMEM((nsub, tb, LANE), _F32)],
        compiler_params=_params("parallel", "parallel", "arbitrary"),
        name="sb_attention",
    )(qkv, qkv, qkv, _suffix_sum_matrix())


def kernel(x, a_norm, a_w_in, a_v_norm, a_w_s, a_b_s, a_w_out, b_norm, b_w_qkv, b_q_norm, b_k_norm, b_w_out, f_norm, f_w_gate, f_w_up, f_w_down):
    batch, seq, d = x.shape
    depth = f_norm.shape[0]
    w_in, w_a_out = _to_bf16(a_w_in), _to_bf16(a_w_out)
    w_qkv, w_b_out = _to_bf16(b_w_qkv), _to_bf16(b_w_out)
    w_down = _to_bf16(f_w_down)
    q_scale = LOG2E / math.sqrt(HEAD_DIM)

    def mixer_gain(i):
        return a_norm[i // 2] if i % 2 == 0 else b_norm[i // 2]

    xr = x.reshape(batch * seq, d)
    xg, ss = _norm_inputs(xr, mixer_gain(0))
    for i in range(depth):
        j = i // 2
        if i % 2 == 0:
            z = _proj_gelu(xg, ss, w_in, j)
            mixed = _spatial_gate(z, a_v_norm[j], a_w_s[j], a_b_s[j])
            xr, xg, ss = _proj_residual(mixed, w_a_out, j, xr, f_norm[i], BLOCK_OUT)
        else:
            qkv = _proj_qkv(xg, ss, w_qkv, j, b_q_norm[j] * q_scale, b_k_norm[j])
            mixed = _sb_attention(qkv, batch, seq)
            xr, xg, ss = _proj_residual(mixed, w_b_out, j, xr, f_norm[i], BLOCK_OUT)
        a = _proj_swiglu(xg, ss, f_w_gate, f_w_up, i)
        if i + 1 < depth:
            xr, xg, ss = _proj_residual(a, w_down, i, xr, mixer_gain(i + 1), BLOCK_DOWN)
        else:
            xr = _proj_residual(a, w_down, i, xr, None, BLOCK_DOWN)
    return xr.reshape(batch, seq, d)
```

```python
import functools
import math

import jax
import jax.numpy as jnp
from jax import lax
from jax.experimental import pallas as pl
from jax.experimental.pallas import tpu as pltpu

RMS_EPS = 1e-6
CHUNK = 64
GMLP_BLOCK = 128
HEAD_DIM = 128
LANE = 128
MXU_COLS = 256
BF16_SUBLANES = 16
V7X_VMEM_BYTES = 64 * 1024 * 1024
VMEM_LIMIT_BYTES = V7X_VMEM_BYTES - 8 * 1024 * 1024
VMEM_LIMIT_OUT_PROJ_BYTES = V7X_VMEM_BYTES - 2 * 1024 * 1024
LOG2E = 1.4426950408889634
EXIT_LOG2 = 151.0

BLOCK_GELU = (1024, 1024)
BLOCK_QKV = (1024, 1024)
BLOCK_OUT = (1024, 1024)
BLOCK_SWIGLU = (1024, 256)
BLOCK_DOWN = (512, 512)

_BF16 = jnp.bfloat16
_F32 = jnp.float32


def _params(*semantics, vmem_limit_bytes=VMEM_LIMIT_BYTES):
    return pltpu.CompilerParams(dimension_semantics=semantics,
                                vmem_limit_bytes=vmem_limit_bytes)


def _dot(a, b):
    return jnp.dot(a, b, preferred_element_type=_F32)


def _lane_partial_sums(sq):
    part = sq[:, :LANE]
    for c in range(1, sq.shape[1] // LANE):
        part = part + sq[:, c * LANE:(c + 1) * LANE]
    return part


def _row_rsqrt(ss_ref, width):
    return lax.rsqrt(jnp.sum(ss_ref[...], axis=-1, keepdims=True) * (1.0 / width) + RMS_EPS)


def _cast_kernel(w_ref, o_ref):
    o_ref[...] = w_ref[...].astype(o_ref.dtype)


def _to_bf16(w, *, block_bytes=4 * 1024 * 1024):
    layers, k, n = w.shape
    bk = BF16_SUBLANES
    while 2 * bk * n * 4 <= block_bytes and k % (2 * bk) == 0:
        bk *= 2
    assert k % bk == 0, (k, bk)
    return pl.pallas_call(
        _cast_kernel,
        grid=(layers, k // bk),
        in_specs=[pl.BlockSpec((None, bk, n), lambda l, i: (l, i, 0))],
        out_specs=pl.BlockSpec((None, bk, n), lambda l, i: (l, i, 0)),
        out_shape=jax.ShapeDtypeStruct(w.shape, _BF16),
        compiler_params=_params("parallel", "parallel"),
        name="cast_bf16",
    )(w)


def _norm_inputs_kernel(x_ref, g_ref, xg_ref, ss_ref):
    x = x_ref[...]
    xg_ref[...] = (x * g_ref[...]).astype(xg_ref.dtype)
    ss_ref[...] = _lane_partial_sums(x * x)


def _norm_inputs(x, gain, *, rows=512):
    m, d = x.shape
    rows = min(rows, m)
    return pl.pallas_call(
        _norm_inputs_kernel,
        grid=(m // rows,),
        in_specs=[pl.BlockSpec((rows, d), lambda i: (i, 0)),
                  pl.BlockSpec((1, d), lambda i: (0, 0))],
        out_specs=[pl.BlockSpec((rows, d), lambda i: (i, 0)),
                   pl.BlockSpec((rows, LANE), lambda i: (i, 0))],
        out_shape=[jax.ShapeDtypeStruct((m, d), _BF16), jax.ShapeDtypeStruct((m, LANE), _F32)],
        compiler_params=_params("parallel"),
        name="norm_inputs",
    )(x, gain.reshape(1, d))


def _w_spec(k, bn, layer):
    return pl.BlockSpec((None, k, bn), lambda i, j: (layer, 0, j))


def _row_spec(bm, width):
    return pl.BlockSpec((bm, width), lambda i, j: (i, 0))


def _tile_spec(bm, bn):
    return pl.BlockSpec((bm, bn), lambda i, j: (i, j))


def _gelu_kernel(xg_ref, ss_ref, w_ref, o_ref):
    acc = _dot(xg_ref[...], w_ref[...]) * _row_rsqrt(ss_ref, xg_ref.shape[1])
    o_ref[...] = (0.5 * acc * (1.0 + lax.erf(acc * math.sqrt(0.5)))).astype(o_ref.dtype)


def _swiglu_kernel(xg_ref, ss_ref, wg_ref, wu_ref, o_ref):
    xg = xg_ref[...]
    r = _row_rsqrt(ss_ref, xg_ref.shape[1])
    g = _dot(xg, wg_ref[...].astype(_BF16)) * r
    u = _dot(xg, wu_ref[...].astype(_BF16)) * r
    o_ref[...] = (g * jax.nn.sigmoid(g) * u).astype(o_ref.dtype)


def _qkv_kernel(xg_ref, ss_ref, w_ref, g_ref, o_ref, *, n_norm_blocks):
    normed_block = pl.program_id(1) < n_norm_blocks
    gain = g_ref[0]
    xg = xg_ref[...]
    r = _row_rsqrt(ss_ref, xg_ref.shape[1])
    for c in range(o_ref.shape[1] // MXU_COLS):
        acc = _dot(xg, w_ref[:, c * MXU_COLS:(c + 1) * MXU_COLS]) * r
        for h in range(MXU_COLS // HEAD_DIM):
            blk = acc[:, h * HEAD_DIM:(h + 1) * HEAD_DIM]
            ms = jnp.mean(blk * blk, axis=-1, keepdims=True)
            normed = blk * lax.rsqrt(ms + RMS_EPS) * gain
            lo = c * MXU_COLS + h * HEAD_DIM
            o_ref[:, lo:lo + HEAD_DIM] = jnp.where(normed_block, normed, blk).astype(o_ref.dtype)


def _residual_kernel(x_ref, w_ref, r_ref, o_ref):
    o_ref[...] = r_ref[...] + _dot(x_ref[...], w_ref[...])


def _residual_norm_kernel(x_ref, w_ref, r_ref, g_ref, o_ref, xg_ref, ss_ref):
    new = r_ref[...] + _dot(x_ref[...], w_ref[...])
    o_ref[...] = new
    xg_ref[...] = (new * g_ref[...]).astype(xg_ref.dtype)
    ss_ref[...] = _lane_partial_sums(new * new)


def _proj_gelu(xg, ss, w, layer):
    m, k = xg.shape
    n = w.shape[2]
    bm, bn = min(BLOCK_GELU[0], m), min(BLOCK_GELU[1], n)
    return pl.pallas_call(
        _gelu_kernel,
        grid=(m // bm, n // bn),
        in_specs=[_row_spec(bm, k), _row_spec(bm, ss.shape[1]), _w_spec(k, bn, layer)],
        out_specs=_tile_spec(bm, bn),
        out_shape=jax.ShapeDtypeStruct((m, n), _BF16),
        compiler_params=_params("parallel", "arbitrary"),
        name="proj_gelu",
    )(xg, ss, w)


def _proj_swiglu(xg, ss, wg, wu, layer):
    m, k = xg.shape
    n = wg.shape[2]
    bm, bn = min(BLOCK_SWIGLU[0], m), min(BLOCK_SWIGLU[1], n)
    return pl.pallas_call(
        _swiglu_kernel,
        grid=(m // bm, n // bn),
        in_specs=[_row_spec(bm, k), _row_spec(bm, ss.shape[1]), _w_spec(k, bn, layer), _w_spec(k, bn, layer)],
        out_specs=_tile_spec(bm, bn),
        out_shape=jax.ShapeDtypeStruct((m, n), _BF16),
        compiler_params=_params("parallel", "arbitrary"),
        name="proj_swiglu",
    )(xg, ss, wg, wu)


def _proj_qkv(xg, ss, w, layer, q_gain, k_gain):
    m, k = xg.shape
    n = w.shape[2]
    third = n // 3
    bm, bn = min(BLOCK_QKV[0], m), min(BLOCK_QKV[1], third)
    per_third = third // bn
    gains = jnp.stack([q_gain, k_gain]).reshape(2, 1, HEAD_DIM)
    return pl.pallas_call(
        functools.partial(_qkv_kernel, n_norm_blocks=2 * per_third),
        grid=(m // bm, n // bn),
        in_specs=[_row_spec(bm, k), _row_spec(bm, ss.shape[1]), _w_spec(k, bn, layer),
                  pl.BlockSpec((1, 1, HEAD_DIM),
                               lambda i, j: (jnp.minimum(j // per_third, 1), 0, 0))],
        out_specs=_tile_spec(bm, bn),
        out_shape=jax.ShapeDtypeStruct((m, n), _BF16),
        compiler_params=_params("parallel", "arbitrary"),
        name="proj_qkv",
    )(xg, ss, w, gains)


def _proj_residual(x, w, layer, res, next_gain, block, vmem_limit_bytes=VMEM_LIMIT_BYTES):
    m, k = x.shape
    n = w.shape[2]
    bm, bn = min(block[0], m), min(block[1], n)
    in_specs = [_row_spec(bm, k), _w_spec(k, bn, layer), _tile_spec(bm, bn)]
    if next_gain is None:
        return pl.pallas_call(
            _residual_kernel,
            grid=(m // bm, n // bn),
            in_specs=in_specs,
            out_specs=_tile_spec(bm, bn),
            out_shape=jax.ShapeDtypeStruct((m, n), _F32),
            compiler_params=_params("parallel", "arbitrary"),
            name="proj_residual",
        )(x, w, res)
    return pl.pallas_call(
        _residual_norm_kernel,
        grid=(m // bm, n // bn),
        in_specs=in_specs + [pl.BlockSpec((1, bn), lambda i, j: (0, j))],
        out_specs=[_tile_spec(bm, bn), _tile_spec(bm, bn), _tile_spec(bm, LANE)],
        out_shape=[jax.ShapeDtypeStruct((m, n), _F32), jax.ShapeDtypeStruct((m, n), _BF16),
                   jax.ShapeDtypeStruct((m, (n // bn) * LANE), _F32)],
        compiler_params=_params("parallel", "arbitrary", vmem_limit_bytes=vmem_limit_bytes),
        name="proj_residual_norm",
    )(x, w, res, next_gain.reshape(1, n))


def _spatial_kernel(u_ref, v_ref, vg_ref, w_ref, b_ref, o_ref):
    rows, width = v_ref.shape
    v = v_ref[...].astype(_F32)
    ms = jnp.mean(v * v, axis=-1, keepdims=True)
    vn = (v * lax.rsqrt(ms + RMS_EPS) * vg_ref[...]).astype(_BF16)
    t_chunk = lax.broadcasted_iota(jnp.int32, (GMLP_BLOCK, GMLP_BLOCK), 0) // CHUNK
    s_chunk = lax.broadcasted_iota(jnp.int32, (GMLP_BLOCK, GMLP_BLOCK), 1) // CHUNK
    causal = s_chunk <= t_chunk
    for g in range(width // HEAD_DIM):
        cols = slice(g * HEAD_DIM, (g + 1) * HEAD_DIM)
        w = jnp.where(causal, w_ref[g], 0.0).astype(_BF16)
        for p in range(rows // GMLP_BLOCK):
            rws = slice(p * GMLP_BLOCK, (p + 1) * GMLP_BLOCK)
            sv = _dot(w, vn[rws, cols]) + b_ref[:, cols]
            o_ref[rws, cols] = (u_ref[rws, cols].astype(_F32) * sv).astype(o_ref.dtype)


def _spatial_gate(z, v_gain, w_s, b_s, *, rows=256):
    m, two_w = z.shape
    width = two_w // 2
    rows = min(rows, m)
    groups = width // HEAD_DIM
    b_full = jnp.repeat(b_s.T, HEAD_DIM, axis=1)
    return pl.pallas_call(
        _spatial_kernel,
        grid=(m // rows,),
        in_specs=[pl.BlockSpec((rows, width), lambda i: (i, 0)),
                  pl.BlockSpec((rows, width), lambda i: (i, 1)),
                  pl.BlockSpec((1, width), lambda i: (0, 0)),
                  pl.BlockSpec((groups, GMLP_BLOCK, GMLP_BLOCK), lambda i: (0, 0, 0)),
                  pl.BlockSpec((GMLP_BLOCK, width), lambda i: (0, 0))],
        out_specs=pl.BlockSpec((rows, width), lambda i: (i, 0)),
        out_shape=jax.ShapeDtypeStruct((m, width), _BF16),
        compiler_params=_params("parallel"),
        name="gmlp_spatial",
    )(z, z, v_gain.reshape(1, width), w_s, b_full)


def _sb_attn_kernel(q_ref, k_ref, v_ref, cs_ref, o_ref, acc_ref, carry_ref, *, nsub, tb):
    base = pl.program_id(2) * nsub
    cs = cs_ref[...]
    groups = tb // LANE
    t_idx = lax.broadcasted_iota(jnp.int32, (tb, tb), 0)
    s_idx = lax.broadcasted_iota(jnp.int32, (tb, tb), 1)
    strict = s_idx < t_idx

    def sweep(blocks):
        staged, operands = [], []
        for r, kb, diagonal, valid in blocks:
            q = q_ref[r * tb:(r + 1) * tb, :]
            k0 = pl.multiple_of(kb * tb, tb)
            k = k_ref[pl.ds(k0, tb), :]
            z2 = lax.dot_general(q, k, (((1,), (1,)), ((), ())), preferred_element_type=_F32)
            sp = jnp.maximum(z2, 0.0) + jnp.log2(1.0 + jnp.exp2(-jnp.abs(z2)))
            log_beta = z2 - sp
            if diagonal:
                sp = jnp.where(strict, sp, 0.0)
            if valid is not None:
                sp = jnp.where(valid, sp, 0.0)
            for g in range(groups):
                spg = sp[:, g * LANE:(g + 1) * LANE]
                hi = spg.astype(_BF16)
                lo = (spg - hi.astype(_F32)).astype(_BF16)
                operands.append(jnp.concatenate([hi, lo], axis=1))
            staged.append((r, k0, diagonal, valid, log_beta))
        sums = _dot(jnp.concatenate(operands, axis=0), cs)
        for n, (r, k0, diagonal, valid, log_beta) in enumerate(staged):
            carry = None if diagonal else carry_ref[r]
            a_parts = [None] * groups
            for g in reversed(range(groups)):
                sg = sums[(n * groups + g) * tb:(n * groups + g + 1) * tb, :]
                log_a = log_beta[:, g * LANE:(g + 1) * LANE] - sg[:, :LANE]
                if carry is not None:
                    log_a = log_a - carry
                a = jnp.exp2(log_a)
                if diagonal:
                    a = jnp.where(strict[:, g * LANE:(g + 1) * LANE], a, 0.0)
                if valid is not None:
                    a = jnp.where(valid, a, 0.0)
                a_parts[g] = a.astype(_BF16)
                carry = sg[:, LANE:] if carry is None else carry + sg[:, LANE:]
            carry_ref[r] = carry
            pv = _dot(jnp.concatenate(a_parts, axis=1), v_ref[pl.ds(k0, tb), :])
            if diagonal:
                acc_ref[r] = pv
            else:
                acc_ref[r] += pv

    def finished(r_from):
        return (jnp.min(carry_ref[r_from:]) >= EXIT_LOG2).astype(jnp.int32)

    sweep([(r, base + r, True, None) for r in range(nsub)]
          + [(0, jnp.maximum(base - 1, 0), False, base > 0)]
          + [(r, base + r - 1, False, None) for r in range(1, nsub)])

    def cond(state):
        i, done = state
        return jnp.logical_and(i <= base, done == 0)

    def body(state):
        i, _ = state
        sweep([(r, base + r - i, False, None) for r in range(nsub)])
        return i + 1, finished(0)

    _, done = lax.while_loop(cond, body, (jnp.int32(2), finished(0)))

    @pl.when(done == 0)
    def _():
        for j in range(1, nsub):
            todo = finished(j) == 0
            if j == 1:
                todo = jnp.logical_and(todo, base > 0)

            @pl.when(todo)
            def _():
                sweep([(r, r - j, False, None) for r in range(j, nsub)])

    for r in range(nsub):
        o_ref[r * tb:(r + 1) * tb, :] = acc_ref[r].astype(o_ref.dtype)


def _suffix_sum_matrix():
    j = lax.broadcasted_iota(jnp.int32, (LANE, LANE), 0)
    s = lax.broadcasted_iota(jnp.int32, (LANE, LANE), 1)
    half = jnp.concatenate([(j > s).astype(_BF16), jnp.ones((LANE, LANE), _BF16)], axis=1)
    return jnp.concatenate([half, half], axis=0)


def _sb_attention(qkv, batch, seq, *, nsub=8, tb=256):
    m, three_hd = qkv.shape
    heads = three_hd // (3 * HEAD_DIM)
    tb = min(tb, seq)
    nsub = min(nsub, seq // tb)
    tq = nsub * tb
    nq = seq // tq
    return pl.pallas_call(
        functools.partial(_sb_attn_kernel, nsub=nsub, tb=tb),
        grid=(batch, heads, nq),
        in_specs=[pl.BlockSpec((tq, HEAD_DIM), lambda b, h, i: (b * nq + i, h)),
                  pl.BlockSpec((seq, HEAD_DIM), lambda b, h, i: (b, heads + h)),
                  pl.BlockSpec((seq, HEAD_DIM), lambda b, h, i: (b, 2 * heads + h)),
                  pl.BlockSpec((2 * LANE, 2 * LANE), lambda b, h, i: (0, 0))],
        out_specs=pl.BlockSpec((tq, HEAD_DIM), lambda b, h, i: (b * nq + i, h)),
        out_shape=jax.ShapeDtypeStruct((m, heads * HEAD_DIM), _BF16),
        scratch_shapes=[pltpu.VMEM((nsub, tb, HEAD_DIM), _F32), pltpu.VMEM((nsub, tb, LANE), _F32)],
        compiler_params=_params("parallel", "parallel", "arbitrary"),
        name="sb_attention",
    )(qkv, qkv, qkv, _suffix_sum_matrix())


def kernel(x, a_norm, a_w_in, a_v_norm, a_w_s, a_b_s, a_w_out, b_norm, b_w_qkv, b_q_norm, b_k_norm, b_w_out, f_norm, f_w_gate, f_w_up, f_w_down):
    batch, seq, d = x.shape
    depth = f_norm.shape[0]
    w_in, w_a_out = _to_bf16(a_w_in), _to_bf16(a_w_out)
    w_qkv, w_b_out = _to_bf16(b_w_qkv), _to_bf16(b_w_out)
    w_down = _to_bf16(f_w_down)
    q_scale = LOG2E / math.sqrt(HEAD_DIM)

    def mixer_gain(i):
        return a_norm[i // 2] if i % 2 == 0 else b_norm[i // 2]

    xr = x.reshape(batch * seq, d)
    xg, ss = _norm_inputs(xr, mixer_gain(0))
    for i in range(depth):
        j = i // 2
        if i % 2 == 0:
            z = _proj_gelu(xg, ss, w_in, j)
            mixed = _spatial_gate(z, a_v_norm[j], a_w_s[j], a_b_s[j])
            xr, xg, ss = _proj_residual(mixed, w_a_out, j, xr, f_norm[i], BLOCK_OUT, VMEM_LIMIT_OUT_PROJ_BYTES)
        else:
            qkv = _proj_qkv(xg, ss, w_qkv, j, b_q_norm[j] * q_scale, b_k_norm[j])
            mixed = _sb_attention(qkv, batch, seq)
            xr, xg, ss = _proj_residual(mixed, w_b_out, j, xr, f_norm[i], BLOCK_OUT, VMEM_LIMIT_OUT_PROJ_BYTES)
        a = _proj_swiglu(xg, ss, f_w_gate, f_w_up, i)
        if i + 1 < depth:
            xr, xg, ss = _proj_residual(a, w_down, i, xr, mixer_gain(i + 1), BLOCK_DOWN)
        else:
            xr = _proj_residual(a, w_down, i, xr, None, BLOCK_DOWN)
    return xr.reshape(batch, seq, d)
```
